```python
import jax, jax.numpy as jnp
from jax import lax
import numpy as np

D_MODEL = 1024
BATCH = 8
SEQ = 8192
DEPTH = 1
DEC_BATCH = 32
DEC_SEQ = 16
PAST_LEN = 1024

CHUNK = 64
D_A = 1024
D_B = 512
D_MIX = D_A + D_B
N_HEADS_A = 16
HEAD_DIM_A = D_A // N_HEADS_A
CONV_A = 4
CONV_B = 3
LRU_C = 8.0
EPS = 1e-6
D_PROJ = 2 * D_A + 4 * D_B
SPLITS = (D_A, 2 * D_A, 2 * D_A + D_B, 2 * D_A + 2 * D_B, 2 * D_A + 3 * D_B)

kernel_name = "hymba_rglru_shortconv_stream_step"


def _rmsnorm_f32(x, g):
    xf = x.astype(jnp.float32)
    r = lax.rsqrt(jnp.mean(xf * xf, axis=-1, keepdims=True) + EPS)
    return xf * r * g.astype(jnp.float32)


def _causal_dwconv(x, hist, w):
    W = w.shape[0]
    T = x.shape[1]
    xe = jnp.concatenate([hist.astype(x.dtype), x], axis=1)
    y = w[0] * xe[:, 0:T]
    for k in range(1, W):
        y = y + w[k] * xe[:, k:k + T]
    return y, xe[:, xe.shape[1] - (W - 1):]


def _lru_combine(left, right):
    a1, b1 = left
    a2, b2 = right
    return a1 * a2, a2 * b1 + b2


def _rg_lru(x, h0, w_rg, b_rg, w_ig, b_ig, lam):
    Bn, T, _ = x.shape
    xf = x.astype(jnp.float32)
    xh = xf.reshape(Bn, T, N_HEADS_A, HEAD_DIM_A)
    r = jax.nn.sigmoid(jnp.einsum('bthi,hij->bthj', xh, w_rg.astype(jnp.float32)).reshape(Bn, T, D_A) + b_rg)
    i = jax.nn.sigmoid(jnp.einsum('bthi,hij->bthj', xh, w_ig.astype(jnp.float32)).reshape(Bn, T, D_A) + b_ig)
    log_a = -LRU_C * r * jax.nn.softplus(-lam.astype(jnp.float32))
    a = jnp.exp(log_a)
    b = jnp.sqrt(-jnp.expm1(2.0 * log_a)) * (i * xf)
    b = b.at[:, 0].add(a[:, 0] * h0.astype(jnp.float32))
    _, h = lax.associative_scan(_lru_combine, (a, b), axis=1)
    return h, h[:, -1]


def _layer(x, c, conv_a_hist, h0, conv_b_hist,
           g_norm, w_ada, b_ada, w_in, w_conv_a, b_conv_a,
           w_rg, b_rg, w_ig, b_ig, lam, w_conv_b, w_out):
    dt = x.dtype
    ada = jax.nn.silu(c.astype(jnp.float32)) @ w_ada.astype(jnp.float32) + b_ada
    shift, scale, gate = jnp.split(ada, 3, axis=-1)
    hn = _rmsnorm_f32(x, g_norm) * (1.0 + scale[:, None]) + shift[:, None]
    proj = jnp.einsum('btd,dp->btp', hn.astype(dt), w_in)
    x_a, g_a, x_b, bgate, cgate, g_b = jnp.split(proj, SPLITS, axis=-1)
    xa_c, new_conv_a = _causal_dwconv(x_a, conv_a_hist, w_conv_a)
    h, h_last = _rg_lru(xa_c + b_conv_a, h0, w_rg, b_rg, w_ig, b_ig, lam)
    y_a = h.astype(dt) * jax.nn.silu(g_a)
    u, new_conv_b = _causal_dwconv(cgate * x_b, conv_b_hist, w_conv_b)
    y_b = bgate * u * jax.nn.silu(g_b)
    out = jnp.einsum('btm,md->btd', jnp.concatenate([y_a, y_b], axis=-1), w_out)
    x = (x.astype(jnp.float32) + gate[:, None] * out.astype(jnp.float32)).astype(dt)
    return x, new_conv_a, h_last.astype(h0.dtype), new_conv_b


def setup_inputs(seed: int = 0) -> dict:
    key = jax.random.key(seed)
    ks = jax.random.split(key, 24)
    f32 = jnp.float32
    nrm = lambda k, s, sc: jax.random.normal(k, s, f32) * sc
    a0 = jax.random.uniform(ks[15], (DEPTH, D_A), f32, 0.9, 0.999)
    return {
        "x_prompt": nrm(ks[0], (BATCH, SEQ, D_MODEL), 1.0),
        "x_sample": nrm(ks[1], (DEC_BATCH, DEC_SEQ, D_MODEL), 1.0),
        "c_prompt": nrm(ks[2], (BATCH, D_MODEL), 1.0),
        "c_sample": nrm(ks[3], (DEC_BATCH, D_MODEL), 1.0),
        "cache_conv_a": nrm(ks[4], (DEPTH, DEC_BATCH, CONV_A - 1, D_A), 1.0),
        "state_lru": nrm(ks[5], (DEPTH, DEC_BATCH, D_A), 0.5),
        "cache_conv_b": nrm(ks[6], (DEPTH, DEC_BATCH, CONV_B - 1, D_B), 1.0),
        "g_norm": 1.0 + nrm(ks[7], (DEPTH, D_MODEL), 0.01),
        "w_ada": nrm(ks[8], (DEPTH, D_MODEL, 3 * D_MODEL), 0.5 * D_MODEL ** -0.5),
        "b_ada": nrm(ks[9], (DEPTH, 3 * D_MODEL), 0.01),
        "w_in": nrm(ks[10], (DEPTH, D_MODEL, D_PROJ), D_MODEL ** -0.5),
        "w_conv_a": nrm(ks[11], (DEPTH, CONV_A, D_A), CONV_A ** -0.5),
        "b_conv_a": nrm(ks[12], (DEPTH, D_A), 0.01),
        "w_rg": nrm(ks[13], (DEPTH, N_HEADS_A, HEAD_DIM_A, HEAD_DIM_A), HEAD_DIM_A ** -0.5),
        "b_rg": nrm(ks[14], (DEPTH, D_A), 0.01),
        "w_ig": nrm(ks[16], (DEPTH, N_HEADS_A, HEAD_DIM_A, HEAD_DIM_A), HEAD_DIM_A ** -0.5),
        "b_ig": nrm(ks[17], (DEPTH, D_A), 0.01),
        "lam": jnp.log(a0) - jnp.log1p(-a0),
        "w_conv_b": nrm(ks[18], (DEPTH, CONV_B, D_B), CONV_B ** -0.5),
        "w_out": nrm(ks[19], (DEPTH, D_MIX, D_MODEL), D_MIX ** -0.5),
        "g_final": 1.0 + nrm(ks[20], (D_MODEL,), 0.01),
    }


def reference(x_prompt, x_sample, c_prompt, c_sample, cache_conv_a, state_lru, cache_conv_b,
              g_norm, w_ada, b_ada, w_in, w_conv_a, b_conv_a, w_rg, b_rg, w_ig, b_ig, lam,
              w_conv_b, w_out, g_final):
    xp, xs = x_prompt, x_sample
    ca_p, h_p, cb_p, ca_s, h_s, cb_s = [], [], [], [], [], []
    Bp = x_prompt.shape[0]
    dt = x_prompt.dtype
    for l in range(DEPTH):
        params = (g_norm[l], w_ada[l], b_ada[l], w_in[l], w_conv_a[l], b_conv_a[l],
                  w_rg[l], b_rg[l], w_ig[l], b_ig[l], lam[l], w_conv_b[l], w_out[l])
        xp, a1, h1, b1 = _layer(xp, c_prompt,
                                jnp.zeros((Bp, CONV_A - 1, D_A), dt),
                                jnp.zeros((Bp, D_A), dt),
                                jnp.zeros((Bp, CONV_B - 1, D_B), dt), *params)
        xs, a2, h2, b2 = _layer(xs, c_sample, cache_conv_a[l], state_lru[l], cache_conv_b[l], *params)
        ca_p.append(a1); h_p.append(h1); cb_p.append(b1)
        ca_s.append(a2); h_s.append(h2); cb_s.append(b2)
    y_prompt = _rmsnorm_f32(xp, g_final).astype(dt)
    y_sample = _rmsnorm_f32(xs, g_final).astype(x_sample.dtype)
    conv_a_prompt = jnp.stack(ca_p)
    lru_prompt = jnp.stack(h_p)
    conv_b_prompt = jnp.stack(cb_p)
    conv_a_sample = jnp.stack(ca_s)
    lru_sample = jnp.stack(h_s)
    conv_b_sample = jnp.stack(cb_s)
    return (y_prompt, y_sample, conv_a_prompt, lru_prompt, conv_b_prompt, conv_a_sample, lru_sample, conv_b_sample)
```

```python
import functools

import jax
import jax.numpy as jnp
import numpy as np
from jax import lax
from jax.experimental import pallas as pl
from jax.experimental.pallas import tpu as pltpu

D_MODEL = 1024
D_A = 1024
D_B = 512
D_MIX = D_A + D_B
D_PROJ = 2 * D_A + 4 * D_B
N_HEADS_A = 16
HEAD_DIM_A = D_A // N_HEADS_A
CONV_A = 4
CONV_B = 3
LRU_C = 8.0
EPS = 1e-6

SUBLANES = 8
BF16_ROWS = 16
GATE_CHUNK = 256
N_GATE_CHUNKS = D_A // GATE_CHUNK
HEADS_PER_CHUNK = GATE_CHUNK // HEAD_DIM_A
TM_PROMPT = 256
VMEM_LIMIT_BYTES = 56 * 1024 * 1024

OFF_XA, OFF_GA, OFF_XB, OFF_BG, OFF_CG, OFF_GB = 0, D_A, 2 * D_A, 2 * D_A + D_B, 2 * D_A + 2 * D_B, 2 * D_A + 3 * D_B

PV_GN, PV_GF, PV_NSP, PV_WCA, PV_BCA, PV_BRG, PV_BIG = 0, 1, 2, 3, 7, 8, 9
N_PV = 10

F32 = jnp.float32
BF16 = jnp.bfloat16


def _sigmoid(z):
    return 1.0 / (1.0 + jnp.exp(-z))


def _prep_kernel(c_ref, w_ref, b_ref, lam_ref, ada_ref, nsp_ref):
    c = c_ref[...]
    sc = (c * _sigmoid(c)).astype(BF16)
    ada_ref[...] = jnp.dot(sc, w_ref[...], preferred_element_type=F32) + b_ref[...]
    z = -lam_ref[...]
    nsp_ref[...] = -LRU_C * (jnp.maximum(z, 0.0) + jnp.log1p(jnp.exp(-jnp.abs(z))))


def _layer_kernel(*refs, prompt, tm):
    g_cnt = tm // SUBLANES
    if prompt:
        (x_ref, ada_ref, pv_ref, wcb_ref, p_ref, pt_ref, win_ref, wg_ref, wout_ref,
         y_ref, ca_out, h_out, cb_out,
         hn_s, hnp_s, proj_s, xacf_s, xac_s, gr_s, gi_s, hl_s, acum_s, ymix_s, ynat_s, out_s, modb_s,
         carry_a, carry_b, carry_h) = refs
    else:
        (x_ref, ada_ref, pv_ref, wcb_ref, p_ref, pt_ref, win_ref, wg_ref, wout_ref,
         cah_ref, h0_ref, cbh_ref,
         y_ref, ca_out, h_out, cb_out,
         hn_s, hnp_s, proj_s, xacf_s, xac_s, gr_s, gi_s, hl_s, acum_s, ymix_s, ynat_s, out_s, modb_s) = refs

    if prompt:
        @pl.when(pl.program_id(1) == 0)
        def _():
            carry_a[...] = jnp.zeros_like(carry_a)
            carry_b[...] = jnp.zeros_like(carry_b)
            carry_h[...] = jnp.zeros_like(carry_h)

    n_seg = ada_ref.shape[0]
    seg = tm // n_seg
    row_a = lax.broadcasted_iota(jnp.int32, (SUBLANES, D_A), 0)
    row_b = lax.broadcasted_iota(jnp.int32, (SUBLANES, D_B), 0)

    def rows(g):
        return slice(SUBLANES * g, SUBLANES * (g + 1))

    for s in range(n_seg):
        shift = ada_ref[s:s + 1, 0:D_MODEL]
        scale = ada_ref[s:s + 1, D_MODEL:2 * D_MODEL]
        gate = ada_ref[s:s + 1, 2 * D_MODEL:3 * D_MODEL]
        modb_s[0] = pv_ref[PV_GN] * (1.0 + jnp.broadcast_to(scale, (SUBLANES, D_MODEL)))
        modb_s[1] = jnp.broadcast_to(shift, (SUBLANES, D_MODEL))
        modb_s[2] = jnp.broadcast_to(gate, (SUBLANES, D_MODEL))
        for q in range(s * seg // SUBLANES, (s + 1) * seg // SUBLANES, 2):
            parts = []
            for qq in (q, q + 1):
                xc = x_ref[rows(qq), :]
                rinv = lax.rsqrt(jnp.mean(xc * xc, axis=-1, keepdims=True) + EPS)
                parts.append(xc * rinv * modb_s[0] + modb_s[1])
            hn_s[SUBLANES * q:SUBLANES * (q + 2), :] = jnp.concatenate(parts, axis=0).astype(BF16)
        if n_seg > 1:
            out_s[rows(s), :] = modb_s[2]

    hnp_s[...] = jnp.dot(p_ref[...], hn_s[...], preferred_element_type=F32).astype(BF16)
    for n0 in range(0, D_PROJ, 1024):
        proj_s[:, n0:n0 + 1024] = jnp.dot(hnp_s[...], win_ref[:, n0:n0 + 1024], preferred_element_type=F32)

    def xa(g):
        return proj_s[rows(g), OFF_XA:OFF_XA + D_A]

    if prompt:
        halo_a = [pltpu.roll(jnp.where(row_a == SUBLANES - 1, carry_a[k], xa(g_cnt - 3 + k)), 1, 0)
                  for k in range(CONV_A - 1)]
    else:
        halo_a = [cah_ref[k] for k in range(CONV_A - 1)]

    def ext_a(j):
        return halo_a[j + CONV_A - 1] if j < 0 else xa(j)

    for g in range(0, g_cnt, 2):
        parts = []
        for gg in (g, g + 1):
            acc = pv_ref[PV_WCA] * ext_a(gg - 3)
            for k in range(1, CONV_A):
                acc = acc + pv_ref[PV_WCA + k] * ext_a(gg - 3 + k)
            parts.append(acc + pv_ref[PV_BCA])
        both = jnp.concatenate(parts, axis=0)
        xacf_s[SUBLANES * g:SUBLANES * (g + 2), :] = both
        xac_s[SUBLANES * g:SUBLANES * (g + 2), :] = both.astype(BF16)

    for k in range(CONV_A - 1):
        last = xa(g_cnt - 3 + k)
        if prompt:
            carry_a[k] = last
            ca_out[k:k + 1, :] = last[SUBLANES - 1:SUBLANES, :]
        else:
            ca_out[k] = last

    for c in range(N_GATE_CHUNKS):
        cols = slice(GATE_CHUNK * c, GATE_CHUNK * (c + 1))
        res = jnp.dot(xac_s[:, cols], wg_ref[c], preferred_element_type=F32)
        gr_s[:, cols] = res[:, :GATE_CHUNK]
        gi_s[:, cols] = res[:, GATE_CHUNK:]

    hl = ac = None
    for g in range(g_cnt):
        r = _sigmoid(gr_s[rows(g), :] + pv_ref[PV_BRG])
        i = _sigmoid(gi_s[rows(g), :] + pv_ref[PV_BIG])
        log_a = r * pv_ref[PV_NSP]
        a = jnp.exp(log_a)
        b = jnp.sqrt(jnp.tanh(log_a) * (-1.0 - a * a)) * (i * xacf_s[rows(g), :])
        if g == 0:
            hl, ac = b, a
        else:
            hl = a * hl + b
            ac = a * ac
        hl_s[rows(g), :] = hl
        acum_s[rows(g), :] = ac

    if prompt:
        a_inc, h_inc = ac, hl
        for sft in (1, 2, 4):
            a_sh = jnp.where(row_a >= sft, pltpu.roll(a_inc, sft, 0), 1.0)
            h_sh = jnp.where(row_a >= sft, pltpu.roll(h_inc, sft, 0), 0.0)
            h_inc = a_inc * h_sh + h_inc
            a_inc = a_inc * a_sh
        h_in = carry_h[...]
        end = a_inc * h_in + h_inc
        cin = jnp.where(row_a == 0, h_in, pltpu.roll(end, 1, 0))
        last = end[SUBLANES - 1:SUBLANES, :]
        carry_h[...] = jnp.broadcast_to(last, (SUBLANES, D_A))
        h_out[...] = last
    else:
        cin = h0_ref[...]
        h_out[...] = ac * cin + hl

    for g in range(0, g_cnt, 2):
        parts = []
        for gg in (g, g + 1):
            h = hl_s[rows(gg), :] + acum_s[rows(gg), :] * cin
            ga = proj_s[rows(gg), OFF_GA:OFF_GA + D_A]
            parts.append(h * (ga * _sigmoid(ga)))
        ymix_s[SUBLANES * g:SUBLANES * (g + 2), 0:D_A] = jnp.concatenate(parts, axis=0).astype(BF16)

    v_cache = {}

    def vb(g):
        if g not in v_cache:
            v_cache[g] = proj_s[rows(g), OFF_CG:OFF_CG + D_B] * proj_s[rows(g), OFF_XB:OFF_XB + D_B]
        return v_cache[g]

    if prompt:
        halo_b = [pltpu.roll(jnp.where(row_b == SUBLANES - 1, carry_b[k], vb(g_cnt - 2 + k)), 1, 0)
                  for k in range(CONV_B - 1)]
    else:
        halo_b = [cbh_ref[k] for k in range(CONV_B - 1)]
    for k in range(CONV_B - 1):
        last = vb(g_cnt - 2 + k)
        if prompt:
            carry_b[k] = last
            cb_out[k:k + 1, :] = last[SUBLANES - 1:SUBLANES, :]
        else:
            cb_out[k] = last
    v_cache.clear()

    def ext_b(j):
        return halo_b[j + CONV_B - 1] if j < 0 else vb(j)

    for g in range(0, g_cnt, 2):
        parts = []
        for gg in (g, g + 1):
            u = wcb_ref[0] * ext_b(gg - 2) + wcb_ref[1] * ext_b(gg - 1) + wcb_ref[2] * ext_b(gg)
            gb = proj_s[rows(gg), OFF_GB:OFF_GB + D_B]
            parts.append(proj_s[rows(gg), OFF_BG:OFF_BG + D_B] * u * (gb * _sigmoid(gb)))
            v_cache.pop(gg - 2, None)
        ymix_s[SUBLANES * g:SUBLANES * (g + 2), D_A:D_MIX] = jnp.concatenate(parts, axis=0).astype(BF16)

    ynat_s[...] = jnp.dot(pt_ref[...], ymix_s[...], preferred_element_type=F32).astype(BF16)
    if n_seg > 1:
        gates = [out_s[rows(s), :] for s in range(n_seg)]
    out_s[...] = jnp.dot(ynat_s[...], wout_ref[...], preferred_element_type=F32)

    for q in range(tm // SUBLANES):
        gate_b = gates[q * SUBLANES // seg] if n_seg > 1 else modb_s[2]
        xo = x_ref[rows(q), :] + gate_b * out_s[rows(q), :]
        rinv = lax.rsqrt(jnp.mean(xo * xo, axis=-1, keepdims=True) + EPS)
        y_ref[rows(q), :] = xo * rinv * pv_ref[PV_GF]


def _perm_matrices(tm):
    g_cnt = tm // SUBLANES
    p = np.arange(tm)
    n = (p % SUBLANES) * g_cnt + p // SUBLANES
    mat = np.zeros((tm, tm), np.float32)
    mat[p, n] = 1.0
    return jnp.asarray(mat, BF16), jnp.asarray(mat.T, BF16)


def _scratch(tm):
    return [
        pltpu.VMEM((tm, D_MODEL), BF16),
        pltpu.VMEM((tm, D_MODEL), BF16),
        pltpu.VMEM((tm, D_PROJ), F32),
        pltpu.VMEM((tm, D_A), F32),
        pltpu.VMEM((tm, D_A), BF16),
        pltpu.VMEM((tm, D_A), F32),
        pltpu.VMEM((tm, D_A), F32),
        pltpu.VMEM((tm, D_A), F32),
        pltpu.VMEM((tm, D_A), F32),
        pltpu.VMEM((tm, D_MIX), BF16),
        pltpu.VMEM((tm, D_MIX), BF16),
        pltpu.VMEM((tm, D_MODEL), F32),
        pltpu.VMEM((3, SUBLANES, D_MODEL), F32),
    ]


def _const_spec(shape, n_grid):
    zeros = (0,) * len(shape)
    if n_grid == 2:
        return pl.BlockSpec(shape, lambda b, t: zeros)
    return pl.BlockSpec(shape, lambda i: zeros)


def kernel(x_prompt, x_sample, c_prompt, c_sample, cache_conv_a, state_lru, cache_conv_b, g_norm, w_ada, b_ada, w_in, w_conv_a, b_conv_a, w_rg, b_rg, w_ig, b_ig, lam, w_conv_b, w_out, g_final):
    assert w_ada.shape[0] == 1, "single-layer kernel"
    bp, tp, _ = x_prompt.shape
    bs, ts, _ = x_sample.shape
    assert tp % TM_PROMPT == 0 and bs % SUBLANES == 0 and (bp + bs) % SUBLANES == 0
    tm_s = SUBLANES * ts

    n_c = -(-(bp + bs) // BF16_ROWS) * BF16_ROWS
    c_all = jnp.concatenate([c_prompt, c_sample, jnp.zeros((n_c - bp - bs, D_MODEL), F32)], axis=0)
    ada, nsp = pl.pallas_call(
        _prep_kernel,
        grid=(3,),
        in_specs=[
            pl.BlockSpec((n_c, D_MODEL), lambda j: (0, 0)),
            pl.BlockSpec((D_MODEL, D_MODEL), lambda j: (0, j)),
            pl.BlockSpec((1, D_MODEL), lambda j: (0, j)),
            pl.BlockSpec((1, D_A), lambda j: (0, 0)),
        ],
        out_specs=[
            pl.BlockSpec((n_c, D_MODEL), lambda j: (0, j)),
            pl.BlockSpec((1, D_A), lambda j: (0, 0)),
        ],
        out_shape=[jax.ShapeDtypeStruct((n_c, 3 * D_MODEL), F32), jax.ShapeDtypeStruct((1, D_A), F32)],
        compiler_params=pltpu.CompilerParams(dimension_semantics=("arbitrary",)),
        name="adaln_prep",
    )(c_all, w_ada[0].astype(BF16), b_ada, lam)

    def bc8(v):
        return jnp.broadcast_to(v.reshape(1, -1), (SUBLANES, v.size))

    pv = jnp.stack([bc8(g_norm[0]), bc8(g_final), bc8(nsp)] + [bc8(w_conv_a[0, k]) for k in range(CONV_A)]
                   + [bc8(b_conv_a[0]), bc8(b_rg[0]), bc8(b_ig[0])])
    wcb = jnp.stack([bc8(w_conv_b[0, k]) for k in range(CONV_B)])
    eye = jnp.eye(HEADS_PER_CHUNK, dtype=F32)

    def block_diag(w):
        w4 = w.reshape(N_GATE_CHUNKS, HEADS_PER_CHUNK, HEAD_DIM_A, HEAD_DIM_A)
        return jnp.einsum('chij,hk->chikj', w4, eye).reshape(N_GATE_CHUNKS, GATE_CHUNK, GATE_CHUNK)

    wg = jnp.concatenate([block_diag(w_rg[0]), block_diag(w_ig[0])], axis=-1).astype(BF16)
    win = w_in[0].astype(BF16)
    wout = w_out[0].astype(BF16)

    def weight_specs(tm, n_grid):
        return [
            _const_spec((N_PV, SUBLANES, D_A), n_grid),
            _const_spec((CONV_B, SUBLANES, D_B), n_grid),
            _const_spec((tm, tm), n_grid),
            _const_spec((tm, tm), n_grid),
            _const_spec((D_MODEL, D_PROJ), n_grid),
            _const_spec((N_GATE_CHUNKS, GATE_CHUNK, 2 * GATE_CHUNK), n_grid),
            _const_spec((D_MIX, D_MODEL), n_grid),
        ]

    p_mat, pt_mat = _perm_matrices(TM_PROMPT)
    y_p, ca_p, h_p, cb_p = pl.pallas_call(
        functools.partial(_layer_kernel, prompt=True, tm=TM_PROMPT),
        grid=(bp, tp // TM_PROMPT),
        in_specs=[
            pl.BlockSpec((None, TM_PROMPT, D_MODEL), lambda b, t: (b, t, 0)),
            pl.BlockSpec((None, 1, 3 * D_MODEL), lambda b, t: (b, 0, 0)),
        ] + weight_specs(TM_PROMPT, 2),
        out_specs=[
            pl.BlockSpec((None, TM_PROMPT, D_MODEL), lambda b, t: (b, t, 0)),
            pl.BlockSpec((None, CONV_A - 1, D_A), lambda b, t: (b, 0, 0)),
            pl.BlockSpec((None, 1, D_A), lambda b, t: (b, 0, 0)),
            pl.BlockSpec((None, CONV_B - 1, D_B), lambda b, t: (b, 0, 0)),
        ],
        out_shape=[
            jax.ShapeDtypeStruct((bp, tp, D_MODEL), F32),
            jax.ShapeDtypeStruct((bp, CONV_A - 1, D_A), F32),
            jax.ShapeDtypeStruct((bp, 1, D_A), F32),
            jax.ShapeDtypeStruct((bp, CONV_B - 1, D_B), F32),
        ],
        scratch_shapes=_scratch(TM_PROMPT) + [
            pltpu.VMEM((CONV_A - 1, SUBLANES, D_A), F32),
            pltpu.VMEM((CONV_B - 1, SUBLANES, D_B), F32),
            pltpu.VMEM((SUBLANES, D_A), F32),
        ],
        compiler_params=pltpu.CompilerParams(
            dimension_semantics=("arbitrary", "arbitrary"), vmem_limit_bytes=VMEM_LIMIT_BYTES),
        name="layer_prompt",
    )(x_prompt, ada[:bp].reshape(bp, 1, 3 * D_MODEL), pv, wcb, p_mat, pt_mat, win, wg, wout)

    ps_mat, pts_mat = _perm_matrices(tm_s)
    n_tiles = bs // SUBLANES
    y_s, ca_s, h_s, cb_s = pl.pallas_call(
        functools.partial(_layer_kernel, prompt=False, tm=tm_s),
        grid=(n_tiles,),
        in_specs=[
            pl.BlockSpec((tm_s, D_MODEL), lambda i: (i, 0)),
            pl.BlockSpec((SUBLANES, 3 * D_MODEL), lambda i: (i, 0)),
        ] + weight_specs(tm_s, 1) + [
            pl.BlockSpec((CONV_A - 1, SUBLANES, D_A), lambda i: (0, i, 0)),
            pl.BlockSpec((SUBLANES, D_A), lambda i: (i, 0)),
            pl.BlockSpec((CONV_B - 1, SUBLANES, D_B), lambda i: (0, i, 0)),
        ],
        out_specs=[
            pl.BlockSpec((tm_s, D_MODEL), lambda i: (i, 0)),
            pl.BlockSpec((CONV_A - 1, SUBLANES, D_A), lambda i: (0, i, 0)),
            pl.BlockSpec((SUBLANES, D_A), lambda i: (i, 0)),
            pl.BlockSpec((CONV_B - 1, SUBLANES, D_B), lambda i: (0, i, 0)),
        ],
        out_shape=[
            jax.ShapeDtypeStruct((bs * ts, D_MODEL), F32),
            jax.ShapeDtypeStruct((CONV_A - 1, bs, D_A), F32),
            jax.ShapeDtypeStruct((bs, D_A), F32),
            jax.ShapeDtypeStruct((CONV_B - 1, bs, D_B), F32),
        ],
        scratch_shapes=_scratch(tm_s),
        compiler_params=pltpu.CompilerParams(
            dimension_semantics=("arbitrary",), vmem_limit_bytes=VMEM_LIMIT_BYTES),
        name="layer_sample",
    )(x_sample.reshape(bs * ts, D_MODEL), ada[bp:bp + bs], pv, wcb, ps_mat, pts_mat, win, wg, wout,
      jnp.swapaxes(cache_conv_a[0], 0, 1), state_lru[0], jnp.swapaxes(cache_conv_b[0], 0, 1))

    return (y_p, y_s.reshape(bs, ts, D_MODEL),
            ca_p[None], h_p.reshape(1, bp, D_A), cb_p[None],
            jnp.swapaxes(ca_s, 0, 1)[None], h_s[None], jnp.swapaxes(cb_s, 0, 1)[None])
```
